```python
import math
import jax, jax.numpy as jnp
from jax import lax
import numpy as np

D_MODEL = 1024
BATCH = 16
SEQ = 2048
DEPTH = 1
DEC_BATCH = 32
DEC_SEQ = 4
PAST_LEN = 16384
PAGE_SIZE = 128

HEAD_DIM = 64
N_MOBA_HEADS = D_MODEL // 2 // HEAD_DIM
N_DIFF_HEADS = D_MODEL // 2 // (2 * HEAD_DIM)
DIFF_V_DIM = 2 * HEAD_DIM
N_HEADS_TOTAL = N_MOBA_HEADS + N_DIFF_HEADS
MOBA_W = N_MOBA_HEADS * HEAD_DIM
DIFF_QK_W = N_DIFF_HEADS * 2 * HEAD_DIM
DIFF_V_W = N_DIFF_HEADS * DIFF_V_DIM
MIX_W = MOBA_W + DIFF_V_W
IN_SPLITS = (MOBA_W, 2 * MOBA_W, 3 * MOBA_W, 3 * MOBA_W + DIFF_QK_W, 3 * MOBA_W + 2 * DIFF_QK_W)
MOBA_BLOCK = 256
MOBA_TOPK = 3
MOBA_Q_CHUNK = 8
ATTN_Q_BLOCK = 128
NUM_BUCKETS = 32
MAX_DISTANCE = 128
N_EXPERTS = 32
TOP_K = 4
D_FF = D_MODEL
SWIGLU_LIMIT = 7.0
SWIGLU_ALPHA = 1.702
MOE_BLOCK = 128
LN_EPS = 1e-5
DEEPNORM_ALPHA = (2 * DEPTH) ** 0.25
DEEPNORM_BETA = (8 * DEPTH) ** -0.25
NEG = -1e30

kernel_name = 'hymba_moba_diffattn_moe_decoder_step'


def layer_norm(x, g, b):
    xf = x.astype(jnp.float32)
    mu = jnp.mean(xf, axis=-1, keepdims=True)
    var = jnp.mean(jnp.square(xf - mu), axis=-1, keepdims=True)
    return ((xf - mu) * lax.rsqrt(var + LN_EPS) * g + b).astype(x.dtype)


def t5_bucket(dist):
    n = jnp.maximum(dist, 0)
    max_exact = NUM_BUCKETS // 2
    nf = jnp.maximum(n, 1).astype(jnp.float32)
    large = max_exact + (jnp.log(nf / max_exact) / math.log(MAX_DISTANCE / max_exact)
                         * (NUM_BUCKETS - max_exact)).astype(jnp.int32)
    large = jnp.minimum(large, NUM_BUCKETS - 1)
    return jnp.where(n < max_exact, n, large)


def gather_pages(cache_l, page_table):
    g = cache_l[page_table]
    return g.reshape(g.shape[0], -1, *g.shape[3:])


def to_blocks(parts):
    B, _, H, D = parts[-1].shape
    L = sum(p.shape[1] for p in parts)
    pad = (-L) % MOBA_BLOCK
    full = jnp.concatenate(list(parts) + [jnp.zeros((B, pad, H, D), parts[-1].dtype)], axis=1)
    return full.reshape(B, -1, MOBA_BLOCK, H, D)


def moba_chunk(q_c, qpos, kb, vb, kmean, bias_hb):
    B, NB, BS, H, D = kb.shape
    QC = q_c.shape[1]
    k_sel = min(MOBA_TOPK, NB)
    scale = HEAD_DIM ** -0.5
    own = qpos[0] // MOBA_BLOCK
    gate = jnp.einsum('bqhd,bnhd->bhqn', q_c.astype(jnp.float32), kmean)
    gate = jnp.where(jnp.arange(NB) < own, gate, NEG)
    _, sel = lax.top_k(gate, k_sel)
    slot_ok = jnp.arange(k_sel) < own
    bi = jnp.arange(B)[:, None, None, None]
    hi = jnp.arange(H)[None, :, None, None]
    kg = kb[bi, sel, :, hi]
    vg = vb[bi, sel, :, hi]
    kpos = sel[..., None] * MOBA_BLOCK + jnp.arange(MOBA_BLOCK)
    dist = qpos[None, None, :, None, None] - kpos
    lg = (jnp.einsum('bqhd,bhqsnd->bhqsn', q_c, kg).astype(jnp.float32) * scale
          + bias_hb[hi[..., None], t5_bucket(dist)])
    lg = jnp.where(slot_ok[:, None], lg, NEG).reshape(B, H, QC, k_sel * BS)
    k_own = lax.dynamic_index_in_dim(kb, own, axis=1, keepdims=False)
    v_own = lax.dynamic_index_in_dim(vb, own, axis=1, keepdims=False)
    dist_o = qpos[:, None] - (own * MOBA_BLOCK + jnp.arange(MOBA_BLOCK))[None, :]
    lo = (jnp.einsum('bqhd,bnhd->bhqn', q_c, k_own).astype(jnp.float32) * scale
          + bias_hb[:, t5_bucket(dist_o)])
    lo = jnp.where(dist_o >= 0, lo, NEG)
    p = jax.nn.softmax(jnp.concatenate([lg, lo], axis=-1), axis=-1).astype(vb.dtype)
    p_sel = p[..., :k_sel * BS].reshape(B, H, QC, k_sel, BS)
    return (jnp.einsum('bhqsn,bhqsnd->bqhd', p_sel, vg)
            + jnp.einsum('bhqn,bnhd->bqhd', p[..., k_sel * BS:], v_own))


def moba_attend(q, pos0, kb, vb, bias_hb):
    B, T, H, D = q.shape
    kmean = jnp.mean(kb.astype(jnp.float32), axis=2)
    qc = math.gcd(T, MOBA_Q_CHUNK)
    nc = T // qc
    q_ch = q.reshape(B, nc, qc, H, D).swapaxes(0, 1)
    pos_ch = (pos0 + jnp.arange(T, dtype=jnp.int32)).reshape(nc, qc)
    out = lax.map(lambda a: moba_chunk(a[0], a[1], kb, vb, kmean, bias_hb), (q_ch, pos_ch))
    return out.swapaxes(0, 1).reshape(B, T, H, D)


def diff_block(q_b, qpos, k_all, v_all, lam, bias_hb):
    L = k_all.shape[1]
    dist = qpos[:, None] - jnp.arange(L, dtype=jnp.int32)[None, :]
    lg = (jnp.einsum('bqhmd,bkhmd->bhmqk', q_b, k_all).astype(jnp.float32) * HEAD_DIM ** -0.5
          + bias_hb[:, t5_bucket(dist)][:, None])
    p = jax.nn.softmax(jnp.where(dist >= 0, lg, NEG), axis=-1)
    attn = p[:, :, 0] - lam * p[:, :, 1]
    return jnp.einsum('bhqk,bkhe->bqhe', attn.astype(v_all.dtype), v_all)


def diff_attend(q, pos0, k_all, v_all, lam, bias_hb):
    B, T = q.shape[:2]
    qb = math.gcd(T, ATTN_Q_BLOCK)
    nq = T // qb
    q_ch = q.reshape(B, nq, qb, *q.shape[2:]).swapaxes(0, 1)
    pos_ch = (pos0 + jnp.arange(T, dtype=jnp.int32)).reshape(nq, qb)
    out = lax.map(lambda a: diff_block(a[0], a[1], k_all, v_all, lam, bias_hb), (q_ch, pos_ch))
    return out.swapaxes(0, 1).reshape(B, T, N_DIFF_HEADS, DIFF_V_DIM)


def token_mixers(h, pos0, past, w_in, w_o, lam_q1, lam_k1, lam_q2, lam_k2, subln_g, rel_bias, lam_init):
    B, T, _ = h.shape
    q_m, k_m, v_m, q_d, k_d, v_d = jnp.split(h @ w_in, IN_SPLITS, axis=-1)
    q_m = q_m.reshape(B, T, N_MOBA_HEADS, HEAD_DIM)
    k_m = k_m.reshape(B, T, N_MOBA_HEADS, HEAD_DIM)
    v_m = v_m.reshape(B, T, N_MOBA_HEADS, HEAD_DIM)
    q_d = q_d.reshape(B, T, N_DIFF_HEADS, 2, HEAD_DIM)
    k_d = k_d.reshape(B, T, N_DIFF_HEADS, 2 * HEAD_DIM)
    v_d = v_d.reshape(B, T, N_DIFF_HEADS, DIFF_V_DIM)
    pk_m, pv_m, pk_d, pv_d = ((past[0],), (past[1],), (past[2],), (past[3],)) if past is not None else ((), (), (), ())
    bias_m = rel_bias[:, :N_MOBA_HEADS].T
    bias_d = rel_bias[:, N_MOBA_HEADS:].T
    kb = to_blocks(pk_m + (k_m,))
    vb = to_blocks(pv_m + (v_m,))
    o_m = moba_attend(q_m, pos0, kb, vb, bias_m).reshape(B, T, MOBA_W)
    k_all = jnp.concatenate(pk_d + (k_d,), axis=1).reshape(B, -1, N_DIFF_HEADS, 2, HEAD_DIM)
    v_all = jnp.concatenate(pv_d + (v_d,), axis=1)
    lam = (jnp.exp(jnp.sum(lam_q1 * lam_k1).astype(jnp.float32))
           - jnp.exp(jnp.sum(lam_q2 * lam_k2).astype(jnp.float32)) + lam_init)
    o_d = diff_attend(q_d, pos0, k_all, v_all, lam, bias_d).astype(jnp.float32)
    o_d = o_d * lax.rsqrt(jnp.mean(jnp.square(o_d), axis=-1, keepdims=True) + LN_EPS) * subln_g * (1.0 - lam_init)
    o_d = o_d.astype(h.dtype).reshape(B, T, DIFF_V_W)
    mixed = jnp.concatenate([o_m, o_d], axis=-1) @ w_o
    return mixed, (k_m, v_m, k_d, v_d)


def moe_ffn(h, w_router, b_router, w_gu, b_gu, w_dn, b_dn):
    B, T, D = h.shape
    x = h.reshape(-1, D)
    n = x.shape[0]
    logits = (x @ w_router + b_router).astype(jnp.float32)
    top_v, top_e = lax.top_k(logits, TOP_K)
    gate = jax.nn.softmax(top_v, axis=-1)
    flat_e = top_e.reshape(-1).astype(jnp.int32)
    flat_tok = jnp.arange(n * TOP_K, dtype=jnp.int32) // TOP_K
    flat_w = gate.reshape(-1)
    order = jnp.argsort(flat_e)
    sorted_e = flat_e[order]
    counts = jnp.zeros((N_EXPERTS,), jnp.int32).at[flat_e].add(1)
    padded = (counts + MOE_BLOCK - 1) // MOE_BLOCK * MOE_BLOCK
    pad_end = jnp.cumsum(padded)
    pad_start = pad_end - padded
    start = jnp.cumsum(counts) - counts
    dest = pad_start[sorted_e] + jnp.arange(n * TOP_K, dtype=jnp.int32) - start[sorted_e]
    n_rows = (-(-(n * TOP_K) // MOE_BLOCK) + N_EXPERTS) * MOE_BLOCK
    n_blocks = n_rows // MOE_BLOCK
    row_tok = jnp.zeros((n_rows,), jnp.int32).at[dest].set(flat_tok[order])
    row_w = jnp.zeros((n_rows,), jnp.float32).at[dest].set(flat_w[order])
    blk_e = jnp.minimum(jnp.searchsorted(pad_end, jnp.arange(n_blocks, dtype=jnp.int32) * MOE_BLOCK, side='right'),
                        N_EXPERTS - 1).astype(jnp.int32)

    def expert_block(args):
        tok, e = args
        gu = x[tok] @ w_gu[e] + b_gu[e]
        g, u = jnp.split(gu, 2, axis=-1)
        g = jnp.minimum(g, SWIGLU_LIMIT)
        u = jnp.clip(u, -SWIGLU_LIMIT, SWIGLU_LIMIT)
        act = (u + 1.0) * g * jax.nn.sigmoid(SWIGLU_ALPHA * g)
        return act @ w_dn[e] + b_dn[e]

    rows = lax.map(expert_block, (row_tok.reshape(n_blocks, MOE_BLOCK), blk_e)).reshape(n_rows, D)
    rows = rows * row_w[:, None].astype(rows.dtype)
    return jnp.zeros_like(x).at[row_tok].add(rows).reshape(B, T, D)


def decoder_layer(h, pos0, past, mix_w, moe_w, ln_w):
    ln1_g, ln1_b, ln2_g, ln2_b = ln_w
    a, rows = token_mixers(h, pos0, past, *mix_w)
    h = layer_norm(DEEPNORM_ALPHA * h + a, ln1_g, ln1_b)
    h = layer_norm(DEEPNORM_ALPHA * h + moe_ffn(h, *moe_w), ln2_g, ln2_b)
    return h, rows


def setup_inputs(seed: int = 0) -> dict:
    key = jax.random.key(seed)
    ks = jax.random.split(key, 28)
    f32 = jnp.float32
    n_pages = PAST_LEN // PAGE_SIZE
    n_used = DEC_BATCH * n_pages
    n_phys = n_used + n_used // 4

    def nrm(k, shape, s=1.0):
        return jax.random.normal(k, shape, f32) * s

    in_scale = D_MODEL ** -0.5
    seg_keys = jax.random.split(ks[7], 6)
    widths = (MOBA_W, MOBA_W, MOBA_W, DIFF_QK_W, DIFF_QK_W, DIFF_V_W)
    gains = (1.0, 1.0, DEEPNORM_BETA, 1.0, 1.0, DEEPNORM_BETA)
    w_in = jnp.concatenate([nrm(k, (DEPTH, D_MODEL, w), in_scale * g) for k, w, g in zip(seg_keys, widths, gains)], axis=-1)
    page_table = jax.random.permutation(ks[6], n_phys)[:n_used].reshape(DEC_BATCH, n_pages).astype(jnp.int32)
    return {
        'x_prompt': nrm(ks[0], (BATCH, SEQ, D_MODEL)),
        'x_sample': nrm(ks[1], (DEC_BATCH, DEC_SEQ, D_MODEL)),
        'cache_moba_k': nrm(ks[2], (DEPTH, n_phys, PAGE_SIZE, N_MOBA_HEADS, HEAD_DIM)),
        'cache_moba_v': nrm(ks[3], (DEPTH, n_phys, PAGE_SIZE, N_MOBA_HEADS, HEAD_DIM)),
        'cache_diff_k': nrm(ks[4], (DEPTH, n_phys, PAGE_SIZE, N_DIFF_HEADS, 2 * HEAD_DIM)),
        'cache_diff_v': nrm(ks[5], (DEPTH, n_phys, PAGE_SIZE, N_DIFF_HEADS, DIFF_V_DIM)),
        'page_table': page_table,
        'ln_in_g': 1.0 + nrm(ks[8], (D_MODEL,), 0.02),
        'ln_in_b': nrm(ks[9], (D_MODEL,), 0.02),
        'w_in': w_in,
        'w_o': nrm(ks[10], (DEPTH, MIX_W, D_MODEL), MIX_W ** -0.5 * DEEPNORM_BETA),
        'lambda_q1': nrm(ks[11], (DEPTH, HEAD_DIM), 0.1),
        'lambda_k1': nrm(ks[12], (DEPTH, HEAD_DIM), 0.1),
        'lambda_q2': nrm(ks[13], (DEPTH, HEAD_DIM), 0.1),
        'lambda_k2': nrm(ks[14], (DEPTH, HEAD_DIM), 0.1),
        'subln_g': 1.0 + nrm(ks[15], (DEPTH, DIFF_V_DIM), 0.02),
        'rel_bias': nrm(ks[16], (NUM_BUCKETS, N_HEADS_TOTAL), 0.2),
        'ln1_g': 1.0 + nrm(ks[17], (DEPTH, D_MODEL), 0.02),
        'ln1_b': nrm(ks[18], (DEPTH, D_MODEL), 0.02),
        'w_router': nrm(ks[19], (DEPTH, D_MODEL, N_EXPERTS), in_scale),
        'b_router': nrm(ks[20], (DEPTH, N_EXPERTS), 0.01),
        'w_gate_up': nrm(ks[21], (DEPTH, N_EXPERTS, D_MODEL, 2 * D_FF), in_scale),
        'b_gate_up': nrm(ks[22], (DEPTH, N_EXPERTS, 2 * D_FF), 0.02),
        'w_down': nrm(ks[23], (DEPTH, N_EXPERTS, D_FF, D_MODEL), D_FF ** -0.5 * DEEPNORM_BETA),
        'b_down': nrm(ks[24], (DEPTH, N_EXPERTS, D_MODEL), 0.02),
        'ln2_g': 1.0 + nrm(ks[25], (DEPTH, D_MODEL), 0.02),
        'ln2_b': nrm(ks[26], (DEPTH, D_MODEL), 0.02),
    }


def reference(x_prompt, x_sample, cache_moba_k, cache_moba_v, cache_diff_k, cache_diff_v, page_table,
              ln_in_g, ln_in_b, w_in, w_o, lambda_q1, lambda_k1, lambda_q2, lambda_k2, subln_g, rel_bias,
              ln1_g, ln1_b, w_router, b_router, w_gate_up, b_gate_up, w_down, b_down, ln2_g, ln2_b):
    past_len = page_table.shape[1] * PAGE_SIZE
    hp = layer_norm(x_prompt, ln_in_g, ln_in_b)
    hs = layer_norm(x_sample, ln_in_g, ln_in_b)
    new_p = ([], [], [], [])
    new_s = ([], [], [], [])
    for l in range(DEPTH):
        lam_init = 0.8 - 0.6 * math.exp(-0.3 * l)
        mix_w = (w_in[l], w_o[l], lambda_q1[l], lambda_k1[l], lambda_q2[l], lambda_k2[l], subln_g[l], rel_bias, lam_init)
        moe_w = (w_router[l], b_router[l], w_gate_up[l], b_gate_up[l], w_down[l], b_down[l])
        ln_w = (ln1_g[l], ln1_b[l], ln2_g[l], ln2_b[l])
        past = (gather_pages(cache_moba_k[l], page_table), gather_pages(cache_moba_v[l], page_table),
                gather_pages(cache_diff_k[l], page_table), gather_pages(cache_diff_v[l], page_table))
        hp, rows_p = decoder_layer(hp, 0, None, mix_w, moe_w, ln_w)
        hs, rows_s = decoder_layer(hs, past_len, past, mix_w, moe_w, ln_w)
        for lst, r in zip(new_p, rows_p):
            lst.append(r)
        for lst, r in zip(new_s, rows_s):
            lst.append(r)
    return (hp, hs,
            jnp.stack(new_p[0]), jnp.stack(new_p[1]), jnp.stack(new_p[2]), jnp.stack(new_p[3]),
            jnp.stack(new_s[0]), jnp.stack(new_s[1]), jnp.stack(new_s[2]), jnp.stack(new_s[3]))
```

```python
import functools
import math

import jax
import jax.numpy as jnp
from jax import lax
from jax.experimental import pallas as pl
from jax.experimental.pallas import tpu as pltpu

F32 = jnp.float32
BF16 = jnp.bfloat16

HEAD_DIM = 64
N_MOBA_HEADS = 8
N_DIFF_HEADS = 4
MOBA_W = N_MOBA_HEADS * HEAD_DIM
DIFF_W = N_DIFF_HEADS * 2 * HEAD_DIM
MOBA_BLOCK = 256
MOBA_TOPK = 3
NUM_BUCKETS = 32
MAX_DISTANCE = 128
N_EXPERTS = 32
TOP_K = 4
SWIGLU_LIMIT = 7.0
SWIGLU_ALPHA = 1.702
LN_EPS = 1e-5
NEG = -1e30
QK_SCALE = HEAD_DIM ** -0.5

LANES = 128
SUBLANES = 8
VMEM_LIMIT = 56 * 1024 * 1024

IN_PROJ_ROWS = 256
POST_ATTN_ROWS = 512
EXPERT_ROWS = 256
COMBINE_ROWS = 128
PAGE_ROWS = 128
GROUPS = 8


def _layer_norm(x, g, b):
    mu = jnp.mean(x, axis=-1, keepdims=True)
    xc = x - mu
    var = jnp.mean(xc * xc, axis=-1, keepdims=True)
    return xc * lax.rsqrt(var + LN_EPS) * g + b


def _dot_nt(a, b):
    return lax.dot_general(a, b, (((1,), (1,)), ((), ())), preferred_element_type=F32)


def _split_bf16(x):
    hi = x.astype(BF16)
    lo = (x - hi.astype(F32)).astype(BF16)
    return hi, lo


def _params(*sem):
    return pltpu.CompilerParams(dimension_semantics=sem, vmem_limit_bytes=VMEM_LIMIT)


def _in_proj_kernel(x_ref, g_ref, b_ref, w_ref, qkv_ref, km_ref, vm_ref, kd_ref, vd_ref):
    h = _layer_norm(x_ref[...], g_ref[...], b_ref[...])
    y = jnp.dot(h.astype(BF16), w_ref[...], preferred_element_type=F32)
    w = MOBA_W
    qkv_ref[:, 0:w] = (y[:, 0:w] * QK_SCALE).astype(BF16)
    qkv_ref[:, w:3 * w] = y[:, w:3 * w].astype(BF16)
    qkv_ref[:, 3 * w:4 * w] = (y[:, 3 * w:4 * w] * QK_SCALE).astype(BF16)
    qkv_ref[:, 4 * w:6 * w] = y[:, 4 * w:6 * w].astype(BF16)
    km_ref[...] = y[:, w:2 * w]
    vm_ref[...] = y[:, 2 * w:3 * w]
    kd_ref[...] = y[:, 4 * w:5 * w]
    vd_ref[...] = y[:, 5 * w:6 * w]


def _in_proj(x, g, b, w_bf16):
    n, d = x.shape
    wcols = w_bf16.shape[1]
    tm = min(IN_PROJ_ROWS, n)
    row = lambda i: (i, 0)
    fixed = lambda i: (0, 0)
    kv = jax.ShapeDtypeStruct((n, MOBA_W), F32)
    return pl.pallas_call(
        _in_proj_kernel,
        grid=(n // tm,),
        in_specs=[pl.BlockSpec((tm, d), row), pl.BlockSpec((1, d), fixed), pl.BlockSpec((1, d), fixed),
                  pl.BlockSpec((d, wcols), fixed)],
        out_specs=[pl.BlockSpec((tm, wcols), row)] + [pl.BlockSpec((tm, MOBA_W), row)] * 4,
        out_shape=[jax.ShapeDtypeStruct((n, wcols), BF16), kv, kv, kv, kv],
        compiler_params=_params("parallel"),
        name="in_proj",
    )(x, g, b, w_bf16)


def _t5_bucket(dist):
    n = jnp.maximum(dist, 0)
    max_exact = NUM_BUCKETS // 2
    nf = jnp.maximum(n, 1).astype(F32)
    large = max_exact + (jnp.log(nf / max_exact) / math.log(MAX_DISTANCE / max_exact)
                         * (NUM_BUCKETS - max_exact)).astype(jnp.int32)
    large = jnp.minimum(large, NUM_BUCKETS - 1)
    return jnp.where(n < max_exact, n, large)


def _bias_of(rel_bias, dist):
    b = jnp.moveaxis(rel_bias[_t5_bucket(dist)], -1, 0)
    return jnp.where(dist >= 0, b, NEG)


def _prompt_bias_tiles(rel_bias, blk):
    a = jnp.arange(blk, dtype=jnp.int32)
    d0 = a[:, None] - a[None, :]
    tiles = jnp.stack([_bias_of(rel_bias, d0), _bias_of(rel_bias, d0 + blk)], axis=1)
    far = _bias_of(rel_bias, jnp.full((1, blk), 2 * blk, jnp.int32))
    return tiles, far


def _first_tile(s, v, m_ref, l_ref, acc_ref):
    m = jnp.max(s, axis=-1, keepdims=True)
    p = jnp.exp(s - m)
    m_ref[...] = m
    l_ref[...] = jnp.sum(p, axis=-1, keepdims=True)
    acc_ref[...] = jnp.dot(p.astype(BF16), v, preferred_element_type=F32)


def _next_tile(s, v, m_ref, l_ref, acc_ref):
    m_prev = m_ref[...]
    m = jnp.maximum(m_prev, jnp.max(s, axis=-1, keepdims=True))
    alpha = jnp.exp(m_prev - m)
    p = jnp.exp(s - m)
    m_ref[...] = m
    l_ref[...] = alpha * l_ref[...] + jnp.sum(p, axis=-1, keepdims=True)
    acc_ref[...] = alpha * acc_ref[...] + jnp.dot(p.astype(BF16), v, preferred_element_type=F32)


def _attend(qa, a, block_mask, *, qi, k_ref, v_ref, bias_ref, far_ref, m_ref, l_ref, acc_ref, blk):
    ma, la, acca = m_ref.at[a], l_ref.at[a], acc_ref.at[a]
    row0 = pl.multiple_of(qi * blk, blk)
    s = _dot_nt(qa, k_ref[0, pl.ds(row0, blk), :]) + bias_ref[0, a, 0]
    _first_tile(s, v_ref[0, pl.ds(row0, blk), :], ma, la, acca)

    @pl.when(qi >= 1)
    def _():
        r = pl.multiple_of((qi - 1) * blk, blk)
        s1 = _dot_nt(qa, k_ref[0, pl.ds(r, blk), :]) + bias_ref[0, a, 1] + block_mask(qi - 1)
        _next_tile(s1, v_ref[0, pl.ds(r, blk), :], ma, la, acca)

    def far_body(j, carry):
        r = pl.multiple_of(j * blk, blk)
        sj = _dot_nt(qa, k_ref[0, pl.ds(r, blk), :]) + far_ref[0, a] + block_mask(j)
        _next_tile(sj, v_ref[0, pl.ds(r, blk), :], ma, la, acca)
        return carry

    lax.fori_loop(0, jnp.maximum(qi - 1, 0), far_body, 0)


def _mixer_kernel(q_ref, k_ref, v_ref, bias_ref, far_ref, lq1_ref, lk1_ref, lq2_ref, lk2_ref, sg_ref, o_ref,
                  kmean_ref, m_ref, l_ref, acc_ref, *, nb, blk, n_moba_units, lam_init):
    u = pl.program_id(1)
    qi = pl.program_id(2)
    q = q_ref[0]
    lane = lax.broadcasted_iota(jnp.int32, (blk, LANES), 1)
    q_half = [jnp.where((lane // HEAD_DIM) == a, q, jnp.zeros_like(q)) for a in range(2)]
    attend = functools.partial(_attend, qi=qi, k_ref=k_ref, v_ref=v_ref, bias_ref=bias_ref, far_ref=far_ref,
                               m_ref=m_ref, l_ref=l_ref, acc_ref=acc_ref, blk=blk)

    @pl.when(u < n_moba_units)
    def _():
        @pl.when(qi == 0)
        def _():
            kmean_ref[...] = jnp.zeros_like(kmean_ref)
            for n in range(nb):
                kb = k_ref[0, n * blk:(n + 1) * blk, :].astype(F32)
                kmean_ref[n:n + 1, :] = jnp.mean(kb, axis=0, keepdims=True)

        km_hi, km_lo = _split_bf16(kmean_ref[...])
        for a in range(2):
            qa = q_half[a]
            gate = _dot_nt(qa, km_hi) + _dot_nt(qa, km_lo)
            rank = jnp.zeros((blk, LANES), jnp.int32)
            for i in range(nb):
                gi = gate[:, i:i + 1]
                beats = (gi > gate) | ((gi == gate) & (i < lane))
                rank = rank + jnp.where(beats & (i < qi), 1, 0)
            keep = (lane < qi) & (rank < MOBA_TOPK)
            drop = jnp.where(keep, 0.0, NEG)
            attend(qa, a, lambda j: jnp.sum(jnp.where(lane == j, drop, 0.0), axis=-1, keepdims=True))
        o0 = acc_ref[0] * (1.0 / l_ref[0])
        o1 = acc_ref[1] * (1.0 / l_ref[1])
        o_ref[0] = jnp.where(lane < HEAD_DIM, o0, o1).astype(BF16)

    @pl.when(u >= n_moba_units)
    def _():
        for a in range(2):
            attend(q_half[a], a, lambda j: 0.0)
        lam = _lambda(lq1_ref, lk1_ref, lq2_ref, lk2_ref, lam_init)
        o = acc_ref[0] * (1.0 / l_ref[0]) - lam * (acc_ref[1] * (1.0 / l_ref[1]))
        o_ref[0] = _sub_norm(o, sg_ref[...], lam_init).astype(BF16)


def _lambda(lq1_ref, lk1_ref, lq2_ref, lk2_ref, lam_init):
    s1 = jnp.sum(lq1_ref[...] * lk1_ref[...], axis=-1, keepdims=True)
    s2 = jnp.sum(lq2_ref[...] * lk2_ref[...], axis=-1, keepdims=True)
    return jnp.exp(s1) - jnp.exp(s2) + lam_init


def _sub_norm(o, g, lam_init):
    ms = jnp.mean(o * o, axis=-1, keepdims=True)
    return o * lax.rsqrt(ms + LN_EPS) * g * (1.0 - lam_init)


def _prompt_attention(qkv, tiles, far, lam_vecs, subln_g, lam_init, batch, seq):
    blk = MOBA_BLOCK
    nb = seq // blk
    nu = MOBA_W // LANES
    unit_heads = jnp.asarray([[2 * u, 2 * u + 1] for u in range(nu)]
                             + [[N_MOBA_HEADS + h] * 2 for h in range(N_DIFF_HEADS)])
    unit_tiles, unit_far = tiles[unit_heads], far[unit_heads]
    col = lambda off: (lambda u: u + 2 * nu * (u // nu) + off)
    q_col, k_col, v_col = col(0), col(nu), col(2 * nu)
    vec = pl.BlockSpec((1, HEAD_DIM), lambda b, u, i: (0, 0))
    return pl.pallas_call(
        functools.partial(_mixer_kernel, nb=nb, blk=blk, n_moba_units=nu, lam_init=lam_init),
        grid=(batch, 2 * nu, nb),
        in_specs=[pl.BlockSpec((1, blk, LANES), lambda b, u, i: (b, i, q_col(u))),
                  pl.BlockSpec((1, seq, LANES), lambda b, u, i: (b, 0, k_col(u))),
                  pl.BlockSpec((1, seq, LANES), lambda b, u, i: (b, 0, v_col(u))),
                  pl.BlockSpec((1, 2, 2, blk, blk), lambda b, u, i: (u, 0, 0, 0, 0)),
                  pl.BlockSpec((1, 2, 1, blk), lambda b, u, i: (u, 0, 0, 0)),
                  vec, vec, vec, vec,
                  pl.BlockSpec((1, LANES), lambda b, u, i: (0, 0))],
        out_specs=pl.BlockSpec((1, blk, LANES), lambda b, u, i: (b, i, u)),
        out_shape=jax.ShapeDtypeStruct((batch, seq, MOBA_W + DIFF_W), BF16),
        scratch_shapes=[pltpu.VMEM((LANES, LANES), F32), pltpu.VMEM((2, blk, 1), F32),
                        pltpu.VMEM((2, blk, 1), F32), pltpu.VMEM((2, blk, LANES), F32)],
        compiler_params=_params("parallel", "parallel", "arbitrary"),
        name="mixer_attn",
    )(qkv, qkv, qkv, unit_tiles, unit_far, *lam_vecs, subln_g)


def _paged_kernel(pt_ref, qm_ref, qd_ref, ckm_ref, cvm_ref, ckd_ref, cvd_ref, nkm_ref, nvm_ref, nkd_ref, nvd_ref,
                  bm_ref, bd_ref, lq1_ref, lk1_ref, lq2_ref, lk2_ref, sg_ref, o_ref,
                  mall_ref, lall_ref, oall_ref, ksum_ref, md_ref, ld_ref, accd_ref, *, n_pages, lam_init):
    del pt_ref
    p = pl.program_id(1)
    rows, npp = mall_ref.shape
    lane = lax.broadcasted_iota(jnp.int32, (rows, npp), 1)
    far, last, new = slice(0, LANES), slice(LANES, 2 * LANES), slice(2 * LANES, 3 * LANES)

    @pl.when(p == 0)
    def _():
        mall_ref[...] = jnp.zeros_like(mall_ref)
        lall_ref[...] = jnp.zeros_like(lall_ref)
        ksum_ref[...] = jnp.zeros_like(ksum_ref)
        md_ref[...] = jnp.full_like(md_ref, NEG)
        ld_ref[...] = jnp.zeros_like(ld_ref)
        accd_ref[...] = jnp.zeros_like(accd_ref)

    def moba_partial(k_f32, v_f32, bias):
        s = _dot_nt(qm_ref[0], k_f32.astype(BF16)) + bias
        m = jnp.max(s, axis=-1, keepdims=True)
        e = jnp.exp(s - m)
        o = jnp.dot(e.astype(BF16), v_f32.astype(BF16), preferred_element_type=F32)
        return m, jnp.sum(e, axis=-1, keepdims=True), o

    def diff_update(k_f32, v_f32, bias):
        s = _dot_nt(qd_ref[0], k_f32.astype(BF16)) + bias
        _next_tile(s, v_f32.astype(BF16), md_ref, ld_ref, accd_ref)

    @pl.when(p < n_pages)
    def _():
        k_page = ckm_ref[0]
        ksum_ref[pl.ds(p, 1), :] = jnp.sum(k_page, axis=0, keepdims=True)
        is_last = p == n_pages - 1
        m, l, o = moba_partial(k_page, cvm_ref[0], jnp.where(is_last, bm_ref[:, last], bm_ref[:, far]))
        mall_ref[...] = jnp.where(lane == p, m, mall_ref[...])
        lall_ref[...] = jnp.where(lane == p, l, lall_ref[...])
        oall_ref[p] = o
        diff_update(ckd_ref[0], cvd_ref[0], jnp.where(is_last, bd_ref[:, last], bd_ref[:, far]))

    @pl.when(p == n_pages)
    def _():
        m_own, l_own, o_own = moba_partial(nkm_ref[0], nvm_ref[0], bm_ref[:, new])
        diff_update(nkd_ref[0], nvd_ref[0], bd_ref[:, new])

        ks_hi, ks_lo = _split_bf16(ksum_ref[...])
        g = _dot_nt(qm_ref[0], ks_hi) + _dot_nt(qm_ref[0], ks_lo)
        g = g + jnp.where((lane & 1) == 0, pltpu.roll(g, npp - 1, 1), pltpu.roll(g, 1, 1))
        work = jnp.where(lane < n_pages, g, -jnp.inf)
        keep = lane < 0
        for _ in range(MOBA_TOPK):
            mx = jnp.max(work, axis=-1, keepdims=True)
            first = jnp.min(jnp.where(work == mx, lane, npp), axis=-1, keepdims=True)
            hit = ((lane == first) | (lane == first + 1)) & (mx > -jnp.inf)
            keep = keep | hit
            work = jnp.where(hit, -jnp.inf, work)

        mall = mall_ref[...]
        m_tot = jnp.maximum(jnp.max(jnp.where(keep, mall, NEG), axis=-1, keepdims=True), m_own)
        w = jnp.where(keep, jnp.exp(mall - m_tot), 0.0)
        w_own = jnp.exp(m_own - m_tot)
        l_tot = jnp.sum(w * lall_ref[...], axis=-1, keepdims=True) + w_own * l_own
        acc = w_own * o_own
        for pg in range(n_pages):
            acc = acc + w[:, pg:pg + 1] * oall_ref[pg]

        width = acc.shape[1]
        r = lax.broadcasted_iota(jnp.int32, (rows, width), 0)
        c = lax.broadcasted_iota(jnp.int32, (rows, width), 1)
        nq = rows // GROUPS
        own_lanes = (c // HEAD_DIM) == (r % GROUPS)
        o_m = jnp.where(own_lanes, acc * (1.0 / l_tot), 0.0)
        o_ref[0, :, 0:width] = jnp.sum(o_m.reshape(nq, GROUPS, width), axis=1)

        lam = _lambda(lq1_ref, lk1_ref, lq2_ref, lk2_ref, lam_init)
        sign = jnp.where((r % 2) == 0, 1.0, -lam)
        head_lanes = (c // LANES) == ((r % GROUPS) // 2)
        o_d = jnp.where(head_lanes, accd_ref[...] * (1.0 / ld_ref[...]) * sign, 0.0)
        o_d = jnp.sum(o_d.reshape(nq, GROUPS, width), axis=1)
        for h in range(N_DIFF_HEADS):
            seg = o_d[:, h * LANES:(h + 1) * LANES]
            o_ref[0, :, width + h * LANES:width + (h + 1) * LANES] = _sub_norm(seg, sg_ref[...], lam_init)


def _sample_bias_table(rel_bias, group_heads, n_new):
    q = jnp.arange(n_new, dtype=jnp.int32)[:, None]
    j = jnp.arange(LANES, dtype=jnp.int32)[None, :]
    d_far = jnp.full((n_new, LANES), 2 * MAX_DISTANCE, jnp.int32)
    d_last = PAGE_ROWS + q - j
    d_new = jnp.where(j < n_new, q - j, -1)
    tbl = _bias_of(rel_bias, jnp.concatenate([d_far, d_last, d_new], axis=1))
    tbl = jnp.swapaxes(tbl[jnp.asarray(group_heads)], 0, 1)
    return tbl.reshape(n_new * GROUPS, 3 * LANES)


def _block_diag_queries(q):
    b, n_new, w = q.shape
    grp = jnp.arange(w, dtype=jnp.int32)[None, :] // HEAD_DIM == jnp.arange(GROUPS, dtype=jnp.int32)[:, None]
    out = jnp.where(grp[None, None], q[:, :, None, :], jnp.zeros((), q.dtype))
    return out.reshape(b, n_new * GROUPS, w)


def _paged_attention(qkv_s, new_rows, caches, page_table, rel_bias, lam_vecs, subln_g, lam_init):
    batch, n_new, _ = qkv_s.shape
    n_pages = page_table.shape[1]
    assert n_pages % 2 == 0 and n_new <= SUBLANES
    w = MOBA_W
    rows = n_new * GROUPS
    npp = -(-n_pages // LANES) * LANES
    qm = _block_diag_queries(qkv_s[:, :, 0:w])
    qd = _block_diag_queries(qkv_s[:, :, 3 * w:4 * w])
    pad = lambda a: jnp.pad(a, ((0, 0), (0, PAGE_ROWS - n_new), (0, 0)))
    bm = _sample_bias_table(rel_bias, list(range(N_MOBA_HEADS)), n_new)
    bd = _sample_bias_table(rel_bias, [N_MOBA_HEADS + g // 2 for g in range(GROUPS)], n_new)

    page = lambda b, p, pt: (pt[b * n_pages + jnp.minimum(p, n_pages - 1)], 0, 0)
    per_batch = lambda b, p, pt: (b, 0, 0)
    fixed = lambda b, p, pt: (0, 0)
    q_spec = pl.BlockSpec((1, rows, w), per_batch)
    page_spec = pl.BlockSpec((1, PAGE_ROWS, w), page)
    new_spec = pl.BlockSpec((1, PAGE_ROWS, w), per_batch)
    bias_spec = pl.BlockSpec((rows, 3 * LANES), fixed)
    vec = pl.BlockSpec((1, HEAD_DIM), fixed)
    grid_spec = pltpu.PrefetchScalarGridSpec(
        num_scalar_prefetch=1,
        grid=(batch, n_pages + 1),
        in_specs=[q_spec, q_spec] + [page_spec] * 4 + [new_spec] * 4 + [bias_spec, bias_spec]
                 + [vec] * 4 + [pl.BlockSpec((1, LANES), fixed)],
        out_specs=pl.BlockSpec((1, n_new, 2 * w), per_batch),
        scratch_shapes=[pltpu.VMEM((rows, npp), F32), pltpu.VMEM((rows, npp), F32),
                        pltpu.VMEM((n_pages, rows, w), F32), pltpu.VMEM((npp, w), F32),
                        pltpu.VMEM((rows, 1), F32), pltpu.VMEM((rows, 1), F32), pltpu.VMEM((rows, w), F32)],
    )
    return pl.pallas_call(
        functools.partial(_paged_kernel, n_pages=n_pages, lam_init=lam_init),
        grid_spec=grid_spec,
        out_shape=jax.ShapeDtypeStruct((batch, n_new, 2 * w), F32),
        compiler_params=_params("parallel", "arbitrary"),
        name="paged_attn",
    )(page_table.reshape(-1), qm, qd, *caches, *[pad(a) for a in new_rows], bm, bd, *lam_vecs, subln_g)


def _post_attn_kernel(a_ref, x_ref, gi_ref, bi_ref, wo_ref, g1_ref, b1_ref, wrh_ref, wrl_ref, br_ref,
                      h1_ref, te_ref, tg_ref, *, alpha):
    h = _layer_norm(x_ref[...], gi_ref[...], bi_ref[...])
    mixed = jnp.dot(a_ref[...], wo_ref[...], preferred_element_type=F32)
    h1 = _layer_norm(alpha * h + mixed, g1_ref[...], b1_ref[...])
    h1_ref[...] = h1
    hi, lo = _split_bf16(h1)
    w_hi = wrh_ref[...]
    logits = (jnp.dot(hi, w_hi, preferred_element_type=F32) + jnp.dot(hi, wrl_ref[...], preferred_element_type=F32)
              + jnp.dot(lo, w_hi, preferred_element_type=F32) + br_ref[...])
    lane = lax.broadcasted_iota(jnp.int32, logits.shape, 1)
    work = jnp.where(lane < N_EXPERTS, logits, -jnp.inf)
    vals, ids = [], []
    for _ in range(TOP_K):
        mx = jnp.max(work, axis=-1, keepdims=True)
        first = jnp.min(jnp.where(work == mx, lane, LANES), axis=-1, keepdims=True)
        vals.append(mx)
        ids.append(first)
        work = jnp.where(lane == first, -jnp.inf, work)
    es = [jnp.exp(v - vals[0]) for v in vals]
    inv = 1.0 / (es[0] + es[1] + es[2] + es[3])
    te = jnp.zeros(logits.shape, jnp.int32)
    tg = jnp.zeros(logits.shape, F32)
    for k in range(TOP_K):
        te = jnp.where(lane == k, ids[k], te)
        tg = jnp.where(lane == k, es[k] * inv, tg)
    te_ref[...] = te
    tg_ref[...] = tg


def _post_attn(attn, x, ln_in, w_o_bf16, ln1, wr_hi, wr_lo, br, alpha):
    n, d = x.shape
    tm = min(POST_ATTN_ROWS, n)
    row = lambda i: (i, 0)
    fixed = lambda i: (0, 0)
    vec = pl.BlockSpec((1, d), fixed)
    return pl.pallas_call(
        functools.partial(_post_attn_kernel, alpha=alpha),
        grid=(n // tm,),
        in_specs=[pl.BlockSpec((tm, d), row), pl.BlockSpec((tm, d), row), vec, vec,
                  pl.BlockSpec((d, d), fixed), vec, vec,
                  pl.BlockSpec((d, LANES), fixed), pl.BlockSpec((d, LANES), fixed), pl.BlockSpec((1, LANES), fixed)],
        out_specs=[pl.BlockSpec((tm, d), row), pl.BlockSpec((tm, LANES), row), pl.BlockSpec((tm, LANES), row)],
        out_shape=[jax.ShapeDtypeStruct((n, d), F32), jax.ShapeDtypeStruct((n, LANES), jnp.int32),
                   jax.ShapeDtypeStruct((n, LANES), F32)],
        compiler_params=_params("parallel"),
        name="post_attn",
    )(attn, x, *ln_in, w_o_bf16, *ln1, wr_hi, wr_lo, br)


def _dispatch_tables(top_e, tm):
    n = top_e.shape[0]
    onehot = (top_e[:, :, None] == jnp.arange(N_EXPERTS, dtype=jnp.int32)).astype(jnp.int32).sum(axis=1)
    csum = jnp.cumsum(onehot, axis=0)
    counts = csum[-1]
    padded = (counts + tm - 1) // tm * tm
    pad_end = jnp.cumsum(padded)
    pad_start = pad_end - padded
    dest = pad_start[top_e] + jnp.take_along_axis(csum - onehot, top_e, axis=1)
    n_blocks = -(-(n * TOP_K) // tm) + N_EXPERTS
    tok = jnp.repeat(jnp.arange(n, dtype=jnp.int32), TOP_K)
    row_tok = jnp.zeros((n_blocks * tm,), jnp.int32).at[dest.reshape(-1)].set(tok)
    blk_e = jnp.minimum(jnp.searchsorted(pad_end, jnp.arange(n_blocks, dtype=jnp.int32) * tm, side='right'),
                        N_EXPERTS - 1).astype(jnp.int32)
    return dest.astype(jnp.int32), row_tok.reshape(n_blocks, 1, tm), blk_e


def _expert_kernel(be_ref, rt_ref, h_ref, wgu_ref, bgu_ref, wdn_ref, bdn_ref, o_ref,
                   idx_ref, xbuf_ref, isem, rsem, *, tm, n_blocks):
    del be_ref
    i = pl.program_id(0)
    slot = i % 2
    nxt = 1 - slot

    def idx_copy(blk, s):
        return pltpu.make_async_copy(rt_ref.at[blk, 0], idx_ref.at[s], isem.at[s])

    def issue_rows(s):
        def body(r, carry):
            tok = idx_ref[s, r]
            pltpu.make_async_copy(h_ref.at[pl.ds(tok, 1), :], xbuf_ref.at[s, pl.ds(r, 1), :], rsem.at[s]).start()
            return carry
        lax.fori_loop(0, tm, body, 0)

    @pl.when(i == 0)
    def _():
        first = idx_copy(0, 0)
        first.start()
        first.wait()
        issue_rows(0)
        if n_blocks > 1:
            idx_copy(1, 1).start()

    @pl.when(i + 1 < n_blocks)
    def _():
        idx_copy(i + 1, nxt).wait()
        issue_rows(nxt)

    @pl.when(i + 2 < n_blocks)
    def _():
        idx_copy(i + 2, slot).start()

    pltpu.make_async_copy(h_ref.at[pl.ds(0, tm), :], xbuf_ref.at[slot], rsem.at[slot]).wait()

    x = xbuf_ref[slot].astype(BF16)
    gu = jnp.dot(x, wgu_ref[0], preferred_element_type=F32) + bgu_ref[0]
    f = gu.shape[1] // 2
    g = jnp.minimum(gu[:, :f], SWIGLU_LIMIT)
    u = jnp.clip(gu[:, f:], -SWIGLU_LIMIT, SWIGLU_LIMIT)
    act = (u + 1.0) * g * (1.0 / (1.0 + jnp.exp(-SWIGLU_ALPHA * g)))
    o_ref[...] = jnp.dot(act.astype(BF16), wdn_ref[0], preferred_element_type=F32) + bdn_ref[0]


def _experts(h1, row_tok, blk_e, w_gu, b_gu, w_dn, b_dn):
    n, d = h1.shape
    n_blocks, _, tm = row_tok.shape
    f2 = w_gu.shape[2]
    by_expert3 = lambda i, be: (be[i], 0, 0)
    grid_spec = pltpu.PrefetchScalarGridSpec(
        num_scalar_prefetch=1,
        grid=(n_blocks,),
        in_specs=[pl.BlockSpec(memory_space=pltpu.VMEM), pl.BlockSpec(memory_space=pl.ANY),
                  pl.BlockSpec((1, d, f2), by_expert3), pl.BlockSpec((1, 1, f2), by_expert3),
                  pl.BlockSpec((1, f2 // 2, d), by_expert3), pl.BlockSpec((1, 1, d), by_expert3)],
        out_specs=pl.BlockSpec((tm, d), lambda i, be: (i, 0)),
        scratch_shapes=[pltpu.SMEM((2, tm), jnp.int32), pltpu.VMEM((2, tm, d), F32),
                        pltpu.SemaphoreType.DMA((2,)), pltpu.SemaphoreType.DMA((2,))],
    )
    return pl.pallas_call(
        functools.partial(_expert_kernel, tm=tm, n_blocks=n_blocks),
        grid_spec=grid_spec,
        out_shape=jax.ShapeDtypeStruct((n_blocks * tm, d), F32),
        compiler_params=_params("arbitrary"),
        name="experts",
    )(blk_e, row_tok, h1, w_gu, b_gu, w_dn, b_dn)


def _combine_kernel(dt_ref, r_ref, h1_ref, tg_ref, g2_ref, b2_ref, o_ref, idx_ref, rbuf_ref, isem, rsem,
                    *, tm, n_tiles, alpha):
    i = pl.program_id(0)
    slot = i % 2
    nxt = 1 - slot

    def idx_copy(tile, s):
        return pltpu.make_async_copy(dt_ref.at[tile, 0], idx_ref.at[s], isem.at[s])

    def issue_rows(s):
        for k in range(TOP_K):
            def body(r, carry):
                src = idx_ref[s, k * tm + r]
                pltpu.make_async_copy(r_ref.at[pl.ds(src, 1), :], rbuf_ref.at[s, k, pl.ds(r, 1), :],
                                      rsem.at[s]).start()
                return carry
            lax.fori_loop(0, tm, body, 0)

    @pl.when(i == 0)
    def _():
        first = idx_copy(0, 0)
        first.start()
        first.wait()
        issue_rows(0)
        if n_tiles > 1:
            idx_copy(1, 1).start()

    @pl.when(i + 1 < n_tiles)
    def _():
        idx_copy(i + 1, nxt).wait()
        issue_rows(nxt)

    @pl.when(i + 2 < n_tiles)
    def _():
        idx_copy(i + 2, slot).start()

    for k in range(TOP_K):
        pltpu.make_async_copy(r_ref.at[pl.ds(0, tm), :], rbuf_ref.at[slot, k], rsem.at[slot]).wait()

    tg = tg_ref[...]
    moe = tg[:, 0:1] * rbuf_ref[slot, 0]
    for k in range(1, TOP_K):
        moe = moe + tg[:, k:k + 1] * rbuf_ref[slot, k]
    o_ref[...] = _layer_norm(alpha * h1_ref[...] + moe, g2_ref[...], b2_ref[...])


def _combine(dest, rows, h1, tg, ln2, alpha):
    n, d = h1.shape
    tm = min(COMBINE_ROWS, n)
    n_tiles = n // tm
    dt = dest.reshape(n_tiles, tm, TOP_K).swapaxes(1, 2).reshape(n_tiles, 1, TOP_K * tm)
    row = lambda i: (i, 0)
    fixed = lambda i: (0, 0)
    vec = pl.BlockSpec((1, d), fixed)
    return pl.pallas_call(
        functools.partial(_combine_kernel, tm=tm, n_tiles=n_tiles, alpha=alpha),
        grid=(n_tiles,),
        in_specs=[pl.BlockSpec(memory_space=pltpu.VMEM), pl.BlockSpec(memory_space=pl.ANY),
                  pl.BlockSpec((tm, d), row), pl.BlockSpec((tm, LANES), row), vec, vec],
        out_specs=pl.BlockSpec((tm, d), row),
        out_shape=jax.ShapeDtypeStruct((n, d), F32),
        scratch_shapes=[pltpu.SMEM((2, TOP_K * tm), jnp.int32), pltpu.VMEM((2, TOP_K, tm, d), F32),
                        pltpu.SemaphoreType.DMA((2,)), pltpu.SemaphoreType.DMA((2,))],
        compiler_params=_params("arbitrary"),
        name="combine",
    )(dt, rows, h1, tg, *ln2)


def _moe_and_norm(h1, top_e, tg, moe_w, ln2, alpha):
    dest, row_tok, blk_e = _dispatch_tables(top_e[:, :TOP_K], EXPERT_ROWS)
    rows = _experts(h1, row_tok, blk_e, *moe_w)
    return _combine(dest, rows, h1, tg, ln2, alpha)


def kernel(x_prompt, x_sample, cache_moba_k, cache_moba_v, cache_diff_k, cache_diff_v, page_table, ln_in_g, ln_in_b, w_in, w_o, lambda_q1, lambda_k1, lambda_q2, lambda_k2, subln_g, rel_bias, ln1_g, ln1_b, w_router, b_router, w_gate_up, b_gate_up, w_down, b_down, ln2_g, ln2_b):
    depth = w_in.shape[0]
    assert depth == 1, "single-layer trunk only"
    batch, seq, d = x_prompt.shape
    dec_batch, dec_seq, _ = x_sample.shape
    assert seq % MOBA_BLOCK == 0 and page_table.shape[1] * PAGE_ROWS % MOBA_BLOCK == 0
    alpha = (2 * depth) ** 0.25
    lam_init = 0.8 - 0.6 * math.exp(-0.3 * 0)
    vec = lambda a: a.reshape(1, -1)

    ln_in = (vec(ln_in_g), vec(ln_in_b))
    ln1 = (vec(ln1_g[0]), vec(ln1_b[0]))
    ln2 = (vec(ln2_g[0]), vec(ln2_b[0]))
    w_in_b = w_in[0].astype(BF16)
    w_o_b = w_o[0].astype(BF16)
    lam_vecs = (vec(lambda_q1[0]), vec(lambda_k1[0]), vec(lambda_q2[0]), vec(lambda_k2[0]))
    sg = vec(subln_g[0])
    pad_e = ((0, 0), (0, LANES - N_EXPERTS))
    wr_hi, wr_lo = _split_bf16(jnp.pad(w_router[0], pad_e))
    br = jnp.pad(vec(b_router[0]), pad_e)
    moe_w = (w_gate_up[0].astype(BF16), b_gate_up[0][:, None, :], w_down[0].astype(BF16), b_down[0][:, None, :])

    xp = x_prompt.reshape(batch * seq, d)
    qkv_p, km_p, vm_p, kd_p, vd_p = _in_proj(xp, *ln_in, w_in_b)
    tiles, far = _prompt_bias_tiles(rel_bias, MOBA_BLOCK)
    attn_p = _prompt_attention(qkv_p.reshape(batch, seq, -1), tiles, far, lam_vecs, sg, lam_init, batch, seq)
    h1_p, te_p, tg_p = _post_attn(attn_p.reshape(batch * seq, d), xp, ln_in, w_o_b, ln1, wr_hi, wr_lo, br, alpha)
    y_p = _moe_and_norm(h1_p, te_p, tg_p, moe_w, ln2, alpha)

    xs = x_sample.reshape(dec_batch * dec_seq, d)
    qkv_s, km_s, vm_s, kd_s, vd_s = _in_proj(xs, *ln_in, w_in_b)
    new_rows = [a.reshape(dec_batch, dec_seq, MOBA_W) for a in (km_s, vm_s, kd_s, vd_s)]
    caches = [c[0].reshape(c.shape[1], PAGE_ROWS, MOBA_W)
              for c in (cache_moba_k, cache_moba_v, cache_diff_k, cache_diff_v)]
    attn_s = _paged_attention(qkv_s.reshape(dec_batch, dec_seq, -1), new_rows, caches, page_table, rel_bias,
                              lam_vecs, sg, lam_init)
    attn_s = attn_s.reshape(dec_batch * dec_seq, d).astype(BF16)
    h1_s, te_s, tg_s = _post_attn(attn_s, xs, ln_in, w_o_b, ln1, wr_hi, wr_lo, br, alpha)
    y_s = _moe_and_norm(h1_s, te_s, tg_s, moe_w, ln2, alpha)

    mk = lambda a, b_, t: a.reshape(1, b_, t, N_MOBA_HEADS, HEAD_DIM)
    dk = lambda a, b_, t: a.reshape(1, b_, t, N_DIFF_HEADS, 2 * HEAD_DIM)
    return (y_p.reshape(batch, seq, d), y_s.reshape(dec_batch, dec_seq, d),
            mk(km_p, batch, seq), mk(vm_p, batch, seq), dk(kd_p, batch, seq), dk(vd_p, batch, seq),
            mk(km_s, dec_batch, dec_seq), mk(vm_s, dec_batch, dec_seq),
            dk(kd_s, dec_batch, dec_seq), dk(vd_s, dec_batch, dec_seq))
```

```python
import functools
import math

import jax
import jax.numpy as jnp
from jax import lax
from jax.experimental import pallas as pl
from jax.experimental.pallas import tpu as pltpu

F32 = jnp.float32
BF16 = jnp.bfloat16

HEAD_DIM = 64
N_MOBA_HEADS = 8
N_DIFF_HEADS = 4
MOBA_W = N_MOBA_HEADS * HEAD_DIM
DIFF_W = N_DIFF_HEADS * 2 * HEAD_DIM
MOBA_BLOCK = 256
MOBA_TOPK = 3
NUM_BUCKETS = 32
MAX_DISTANCE = 128
N_EXPERTS = 32
TOP_K = 4
SWIGLU_LIMIT = 7.0
SWIGLU_ALPHA = 1.702
LN_EPS = 1e-5
NEG = -1e30
QK_SCALE = HEAD_DIM ** -0.5

LANES = 128
SUBLANES = 8
VMEM_LIMIT = 56 * 1024 * 1024

IN_PROJ_ROWS = 256
POST_ATTN_ROWS = 512
EXPERT_ROWS = 256
COMBINE_ROWS = 128
PAGE_ROWS = 128
PAGE_WORDS = PAGE_ROWS * MOBA_W // LANES
PAGES_PER_STEP = 8
PAGES_PER_BLOCK = MOBA_BLOCK // PAGE_ROWS
ISSUE_UNROLL = 8


def _layer_norm(x, g, b):
    mu = jnp.mean(x, axis=-1, keepdims=True)
    xc = x - mu
    var = jnp.mean(xc * xc, axis=-1, keepdims=True)
    return xc * lax.rsqrt(var + LN_EPS) * g + b


def _dot_nt(a, b):
    return lax.dot_general(a, b, (((1,), (1,)), ((), ())), preferred_element_type=F32)


def _split_bf16(x):
    hi = x.astype(BF16)
    lo = (x - hi.astype(F32)).astype(BF16)
    return hi, lo


def _params(*sem):
    return pltpu.CompilerParams(dimension_semantics=sem, vmem_limit_bytes=VMEM_LIMIT)


def _in_proj_kernel(x_ref, g_ref, b_ref, w_ref, qkv_ref, km_ref, vm_ref, kd_ref, vd_ref):
    h = _layer_norm(x_ref[...], g_ref[...], b_ref[...])
    y = jnp.dot(h.astype(BF16), w_ref[...], preferred_element_type=F32)
    w = MOBA_W
    qkv_ref[:, 0:w] = (y[:, 0:w] * QK_SCALE).astype(BF16)
    qkv_ref[:, w:3 * w] = y[:, w:3 * w].astype(BF16)
    qkv_ref[:, 3 * w:4 * w] = (y[:, 3 * w:4 * w] * QK_SCALE).astype(BF16)
    qkv_ref[:, 4 * w:6 * w] = y[:, 4 * w:6 * w].astype(BF16)
    km_ref[...] = y[:, w:2 * w]
    vm_ref[...] = y[:, 2 * w:3 * w]
    kd_ref[...] = y[:, 4 * w:5 * w]
    vd_ref[...] = y[:, 5 * w:6 * w]


def _in_proj(x, g, b, w_bf16):
    n, d = x.shape
    wcols = w_bf16.shape[1]
    tm = min(IN_PROJ_ROWS, n)
    row = lambda i: (i, 0)
    fixed = lambda i: (0, 0)
    kv = jax.ShapeDtypeStruct((n, MOBA_W), F32)
    return pl.pallas_call(
        _in_proj_kernel,
        grid=(n // tm,),
        in_specs=[pl.BlockSpec((tm, d), row), pl.BlockSpec((1, d), fixed), pl.BlockSpec((1, d), fixed),
                  pl.BlockSpec((d, wcols), fixed)],
        out_specs=[pl.BlockSpec((tm, wcols), row)] + [pl.BlockSpec((tm, MOBA_W), row)] * 4,
        out_shape=[jax.ShapeDtypeStruct((n, wcols), BF16), kv, kv, kv, kv],
        compiler_params=_params("parallel"),
        name="in_proj",
    )(x, g, b, w_bf16)


def _t5_bucket(dist):
    n = jnp.maximum(dist, 0)
    max_exact = NUM_BUCKETS // 2
    nf = jnp.maximum(n, 1).astype(F32)
    large = max_exact + (jnp.log(nf / max_exact) / math.log(MAX_DISTANCE / max_exact)
                         * (NUM_BUCKETS - max_exact)).astype(jnp.int32)
    large = jnp.minimum(large, NUM_BUCKETS - 1)
    return jnp.where(n < max_exact, n, large)


def _bias_of(rel_bias, dist):
    b = jnp.moveaxis(rel_bias[_t5_bucket(dist)], -1, 0)
    return jnp.where(dist >= 0, b, NEG)


def _prompt_bias_tiles(rel_bias, blk):
    a = jnp.arange(blk, dtype=jnp.int32)
    d0 = a[:, None] - a[None, :]
    tiles = jnp.stack([_bias_of(rel_bias, d0), _bias_of(rel_bias, d0 + blk)], axis=1)
    far = _bias_of(rel_bias, jnp.full((1, blk), 2 * blk, jnp.int32))
    return tiles, far


def _attend(qa, a, c, drop, *, k_ref, v_ref, bias_ref, far_ref, s_ref, blk):
    mx = None
    for j in range(c + 1):
        s = _dot_nt(qa, k_ref[0, j * blk:(j + 1) * blk, :])
        if j == c:
            s = s + bias_ref[0, a, 0]
        else:
            s = s + (bias_ref[0, a, 1] if j == c - 1 else far_ref[0, a])
            if drop is not None:
                s = s + drop[:, j:j + 1]
        s_ref[a, j] = s
        t = jnp.maximum(s[:, :LANES], s[:, LANES:])
        mx = t if mx is None else jnp.maximum(mx, t)
    m = jnp.max(mx, axis=-1, keepdims=True)
    lsum = jnp.zeros((blk, LANES), F32)
    acc = jnp.zeros((blk, LANES), F32)
    for j in range(c + 1):
        p = jnp.exp(s_ref[a, j] - m)
        lsum = lsum + (p[:, :LANES] + p[:, LANES:])
        acc = acc + jnp.dot(p.astype(BF16), v_ref[0, j * blk:(j + 1) * blk, :], preferred_element_type=F32)
    return acc * (1.0 / jnp.sum(lsum, axis=-1, keepdims=True))


def _moba_drop(qa, c, km_hi, km_lo, nbp, blk):
    gate_t = (_dot_nt(km_hi, qa) + _dot_nt(km_lo, qa))[0:nbp, :]
    row = lax.broadcasted_iota(jnp.int32, (nbp, blk), 0)
    rank = jnp.zeros((nbp, blk), jnp.int32)
    for i in range(c):
        gi = gate_t[i:i + 1, :]
        beats = (gi > gate_t) | ((gi == gate_t) & (i < row))
        rank = rank + jnp.where(beats, 1, 0)
    keep = (row < c) & (rank < MOBA_TOPK)
    drop_t = jnp.where(keep, 0.0, NEG)
    drop_t = jnp.concatenate([drop_t, jnp.zeros((LANES - nbp, blk), F32)], axis=0)
    return drop_t.T


def _mixer_kernel(q_ref, k_ref, v_ref, bias_ref, far_ref, lq1_ref, lk1_ref, lq2_ref, lk2_ref, sg_ref, o_ref,
                  kmean_ref, s_ref, *, nb, blk, n_moba_units, lam_init):
    u = pl.program_id(1)
    qi = pl.program_id(2)
    nbp = -(-nb // SUBLANES) * SUBLANES
    attend = functools.partial(_attend, k_ref=k_ref, v_ref=v_ref, bias_ref=bias_ref, far_ref=far_ref,
                               s_ref=s_ref, blk=blk)

    def halves():
        q = q_ref[0]
        lane = lax.broadcasted_iota(jnp.int32, (blk, LANES), 1)
        return lane, [jnp.where((lane // HEAD_DIM) == a, q, jnp.zeros_like(q)) for a in range(2)]

    @pl.when((u < n_moba_units) & (qi == 0))
    def _():
        kmean_ref[...] = jnp.zeros_like(kmean_ref)
        for n in range(nb):
            kb = k_ref[0, n * blk:(n + 1) * blk, :].astype(F32)
            kmean_ref[n:n + 1, :] = jnp.mean(kb, axis=0, keepdims=True)

    def moba_step(c):
        lane, q_half = halves()
        outs = []
        for a in range(2):
            drop = None
            if c > MOBA_TOPK:
                km_hi, km_lo = _split_bf16(kmean_ref[...])
                drop = _moba_drop(q_half[a], c, km_hi, km_lo, nbp, blk)
            outs.append(attend(q_half[a], a, c, drop))
        o_ref[0] = jnp.where(lane < HEAD_DIM, outs[0], outs[1]).astype(BF16)

    def diff_step(c):
        _, q_half = halves()
        o1 = attend(q_half[0], 0, c, None)
        o2 = attend(q_half[1], 1, c, None)
        lam = _lambda(lq1_ref, lk1_ref, lq2_ref, lk2_ref, lam_init)
        o_ref[0] = _sub_norm(o1 - lam * o2, sg_ref[...], lam_init).astype(BF16)

    for c in range(nb):
        pl.when((u < n_moba_units) & (qi == c))(functools.partial(moba_step, c))
        pl.when((u >= n_moba_units) & (qi == c))(functools.partial(diff_step, c))


def _lambda(lq1_ref, lk1_ref, lq2_ref, lk2_ref, lam_init):
    s1 = jnp.sum(lq1_ref[...] * lk1_ref[...], axis=-1, keepdims=True)
    s2 = jnp.sum(lq2_ref[...] * lk2_ref[...], axis=-1, keepdims=True)
    return jnp.exp(s1) - jnp.exp(s2) + lam_init


def _sub_norm(o, g, lam_init):
    ms = jnp.mean(o * o, axis=-1, keepdims=True)
    return o * lax.rsqrt(ms + LN_EPS) * g * (1.0 - lam_init)


def _prompt_attention(qkv, tiles, far, lam_vecs, subln_g, lam_init, batch, seq):
    blk = MOBA_BLOCK
    nb = seq // blk
    nu = MOBA_W // LANES
    unit_heads = jnp.asarray([[2 * u, 2 * u + 1] for u in range(nu)]
                             + [[N_MOBA_HEADS + h] * 2 for h in range(N_DIFF_HEADS)])
    unit_tiles, unit_far = tiles[unit_heads], far[unit_heads]
    col = lambda off: (lambda u: u + 2 * nu * (u // nu) + off)
    q_col, k_col, v_col = col(0), col(nu), col(2 * nu)
    vec = pl.BlockSpec((1, HEAD_DIM), lambda b, u, i: (0, 0))
    return pl.pallas_call(
        functools.partial(_mixer_kernel, nb=nb, blk=blk, n_moba_units=nu, lam_init=lam_init),
        grid=(batch, 2 * nu, nb),
        in_specs=[pl.BlockSpec((1, blk, LANES), lambda b, u, i: (b, i, q_col(u))),
                  pl.BlockSpec((1, seq, LANES), lambda b, u, i: (b, 0, k_col(u))),
                  pl.BlockSpec((1, seq, LANES), lambda b, u, i: (b, 0, v_col(u))),
                  pl.BlockSpec((1, 2, 2, blk, blk), lambda b, u, i: (u, 0, 0, 0, 0)),
                  pl.BlockSpec((1, 2, 1, blk), lambda b, u, i: (u, 0, 0, 0)),
                  vec, vec, vec, vec,
                  pl.BlockSpec((1, LANES), lambda b, u, i: (0, 0))],
        out_specs=pl.BlockSpec((1, blk, LANES), lambda b, u, i: (b, i, u)),
        out_shape=jax.ShapeDtypeStruct((batch, seq, MOBA_W + DIFF_W), BF16),
        scratch_shapes=[pltpu.VMEM((LANES, LANES), F32), pltpu.VMEM((2, nb, blk, blk), F32)],
        compiler_params=_params("parallel", "parallel", "arbitrary"),
        name="mixer_attn",
    )(qkv, qkv, qkv, unit_tiles, unit_far, *lam_vecs, subln_g)


def _head_sums(x, nh):
    return jnp.sum(x.reshape(nh, HEAD_DIM // SUBLANES, SUBLANES, LANES), axis=1).reshape(nh * SUBLANES, LANES)


def _moba_gate_kernel(pt_ref, qcol_ref, *refs, n_blocks, pages_per_step):
    del pt_ref
    k_refs = refs[:pages_per_step]
    sel_ref, gate_ref = refs[pages_per_step:]
    j = pl.program_id(1)
    nq = qcol_ref.shape[1]
    rows = nq * N_MOBA_HEADS
    lane = lax.broadcasted_iota(jnp.int32, (rows, LANES), 1)

    @pl.when(j == 0)
    def _():
        gate_ref[...] = jnp.zeros_like(gate_ref)

    blocks_per_step = pages_per_step // PAGES_PER_BLOCK
    for t in range(blocks_per_step):
        cols = []
        for q in range(nq):
            qc = qcol_ref[0, q]
            part = _head_sums(qc * k_refs[PAGES_PER_BLOCK * t][...], N_MOBA_HEADS)
            for pg in range(1, PAGES_PER_BLOCK):
                part = part + _head_sums(qc * k_refs[PAGES_PER_BLOCK * t + pg][...], N_MOBA_HEADS)
            tot = jnp.sum(part, axis=-1, keepdims=True)
            cols.append(jnp.sum(tot.reshape(N_MOBA_HEADS, SUBLANES, 1), axis=1))
        col = jnp.concatenate(cols, axis=0)
        gate_ref[...] = jnp.where(lane == j * blocks_per_step + t, col, gate_ref[...])

    @pl.when(j == pl.num_programs(1) - 1)
    def _():
        work = jnp.where(lane < n_blocks, gate_ref[...], -jnp.inf)
        sel = jnp.zeros((rows, LANES), jnp.int32)
        for t in range(MOBA_TOPK):
            mx = jnp.max(work, axis=-1, keepdims=True)
            first = jnp.min(jnp.where(work == mx, lane, LANES), axis=-1, keepdims=True)
            sel = jnp.where(lane == t, first, sel)
            work = jnp.where(lane == first, -jnp.inf, work)
        sel_ref[0] = sel


def _moba_gate(qcol, kt, page_table):
    batch, nq = qcol.shape[:2]
    n_pages = page_table.shape[1]
    pps = math.gcd(PAGES_PER_STEP, n_pages)
    n_blocks = n_pages // PAGES_PER_BLOCK
    assert pps % PAGES_PER_BLOCK == 0 and MOBA_TOPK <= n_blocks <= LANES
    rows = nq * N_MOBA_HEADS
    page = lambda t: (lambda b, j, pt: (pt[b * n_pages + j * pps + t], 0))
    grid_spec = pltpu.PrefetchScalarGridSpec(
        num_scalar_prefetch=1,
        grid=(batch, n_pages // pps),
        in_specs=[pl.BlockSpec((1, nq, PAGE_WORDS, LANES), lambda b, j, pt: (b, 0, 0, 0))]
                 + [pl.BlockSpec((PAGE_WORDS, LANES), page(t)) for t in range(pps)],
        out_specs=pl.BlockSpec((1, rows, LANES), lambda b, j, pt: (b, 0, 0)),
        scratch_shapes=[pltpu.VMEM((rows, LANES), F32)],
    )
    return pl.pallas_call(
        functools.partial(_moba_gate_kernel, n_blocks=n_blocks, pages_per_step=pps),
        grid_spec=grid_spec,
        out_shape=jax.ShapeDtypeStruct((batch, rows, LANES), jnp.int32),
        compiler_params=_params("parallel", "arbitrary"),
        name="moba_gate",
    )(page_table.reshape(-1), qcol, *([kt] * pps))


def _moba_sample_kernel(pt_ref, sel_ref, qcol_ref, kt_ref, vt_ref, knew_ref, vnew_ref, bias_ref, o_ref,
                        kbuf_ref, vbuf_ref, ksem, vsem, *, n_pages, n_batch):
    b = pl.program_id(0)
    slot = b % 2
    nq = qcol_ref.shape[1]
    rows = nq * N_MOBA_HEADS
    chunks = MOBA_TOPK * PAGES_PER_BLOCK
    far, last, new = slice(0, LANES), slice(LANES, 2 * LANES), slice(2 * LANES, 3 * LANES)

    def gather(bb, s):
        def body(i, carry):
            h = (i // MOBA_TOPK) % N_MOBA_HEADS
            blk = sel_ref[bb * rows * MOBA_TOPK + i]
            for pg in range(PAGES_PER_BLOCK):
                page = pt_ref[bb * n_pages + blk * PAGES_PER_BLOCK + pg]
                r0 = pl.multiple_of((page * N_MOBA_HEADS + h) * HEAD_DIM, HEAD_DIM)
                c = i * PAGES_PER_BLOCK + pg
                pltpu.make_async_copy(kt_ref.at[pl.ds(r0, HEAD_DIM), :], kbuf_ref.at[s, c], ksem.at[s]).start()
                pltpu.make_async_copy(vt_ref.at[pl.ds(r0, HEAD_DIM), :], vbuf_ref.at[s, c], vsem.at[s]).start()
            return carry
        lax.fori_loop(0, rows * MOBA_TOPK, body, 0)

    @pl.when(b == 0)
    def _():
        gather(0, 0)

    @pl.when(b + 1 < n_batch)
    def _():
        gather(b + 1, 1 - slot)

    pltpu.make_async_copy(kbuf_ref.at[slot], kbuf_ref.at[slot], ksem.at[slot]).wait()
    pltpu.make_async_copy(vbuf_ref.at[slot], vbuf_ref.at[slot], vsem.at[slot]).wait()

    out_lane = lax.broadcasted_iota(jnp.int32, (HEAD_DIM, LANES), 1)

    def row_body(r, out):
        q = r // N_MOBA_HEADS
        h0 = pl.multiple_of((r % N_MOBA_HEADS) * HEAD_DIM, HEAD_DIM)
        qc = qcol_ref[0, q, pl.ds(h0, HEAD_DIM), :]
        brow = bias_ref[pl.ds(r, 1), :]
        scores = []
        for c in range(chunks):
            blk = sel_ref[b * rows * MOBA_TOPK + r * MOBA_TOPK + c // PAGES_PER_BLOCK]
            is_last = blk * PAGES_PER_BLOCK + c % PAGES_PER_BLOCK == n_pages - 1
            s = jnp.sum(qc * kbuf_ref[slot, r * chunks + c], axis=0, keepdims=True)
            scores.append(s + jnp.where(is_last, brow[:, last], brow[:, far]))
        scores.append(jnp.sum(qc * knew_ref[0, pl.ds(h0, HEAD_DIM), :], axis=0, keepdims=True) + brow[:, new])
        mx = scores[0]
        for s in scores[1:]:
            mx = jnp.maximum(mx, s)
        m = jnp.max(mx, axis=-1, keepdims=True)
        ps = [jnp.exp(s - m) for s in scores]
        tot = ps[0]
        for p in ps[1:]:
            tot = tot + p
        l = jnp.sum(tot, axis=-1, keepdims=True)
        acc = ps[chunks] * vnew_ref[0, pl.ds(h0, HEAD_DIM), :]
        for c in range(chunks):
            acc = acc + ps[c] * vbuf_ref[slot, r * chunks + c]
        ocol = jnp.sum(acc, axis=-1, keepdims=True) * (1.0 / l)
        return jnp.where(out_lane == r, ocol, out)

    o_ref[0] = lax.fori_loop(0, rows, row_body, jnp.zeros((HEAD_DIM, LANES), F32))


def _moba_sample(sel, qcol, kt, vt, knew_t, vnew_t, bias, page_table):
    batch, nq = qcol.shape[:2]
    n_pages = page_table.shape[1]
    rows = nq * N_MOBA_HEADS
    assert rows <= LANES
    n_chunks = rows * MOBA_TOPK * PAGES_PER_BLOCK
    per_batch3 = lambda b, pt, sl: (b, 0, 0)
    grid_spec = pltpu.PrefetchScalarGridSpec(
        num_scalar_prefetch=2,
        grid=(batch,),
        in_specs=[pl.BlockSpec((1, nq, PAGE_WORDS, LANES), lambda b, pt, sl: (b, 0, 0, 0)),
                  pl.BlockSpec(memory_space=pl.ANY), pl.BlockSpec(memory_space=pl.ANY),
                  pl.BlockSpec((1, PAGE_WORDS, LANES), per_batch3), pl.BlockSpec((1, PAGE_WORDS, LANES), per_batch3),
                  pl.BlockSpec((rows, 3 * LANES), lambda b, pt, sl: (0, 0))],
        out_specs=pl.BlockSpec((1, HEAD_DIM, LANES), per_batch3),
        scratch_shapes=[pltpu.VMEM((2, n_chunks, HEAD_DIM, LANES), F32), pltpu.VMEM((2, n_chunks, HEAD_DIM, LANES), F32),
                        pltpu.SemaphoreType.DMA((2,)), pltpu.SemaphoreType.DMA((2,))],
    )
    out_t = pl.pallas_call(
        functools.partial(_moba_sample_kernel, n_pages=n_pages, n_batch=batch),
        grid_spec=grid_spec,
        out_shape=jax.ShapeDtypeStruct((batch, HEAD_DIM, LANES), F32),
        compiler_params=_params("arbitrary"),
        name="moba_sample",
    )(page_table.reshape(-1), sel[:, :, :MOBA_TOPK].reshape(-1), qcol, kt, vt, knew_t, vnew_t, bias)
    return jnp.swapaxes(out_t[:, :, :rows], 1, 2).reshape(batch, nq, MOBA_W)


def _sample_bias_rows(rel_bias, heads, n_new):
    q = jnp.arange(n_new, dtype=jnp.int32)[:, None]
    j = jnp.arange(LANES, dtype=jnp.int32)[None, :]
    d_far = jnp.full((n_new, LANES), 2 * MAX_DISTANCE, jnp.int32)
    d_last = PAGE_ROWS + q - j
    d_new = jnp.where(j < n_new, q - j, -1)
    tbl = _bias_of(rel_bias, jnp.concatenate([d_far, d_last, d_new], axis=1))
    tbl = jnp.swapaxes(tbl[jnp.asarray(heads)], 0, 1)
    return tbl.reshape(n_new * len(heads), 3 * LANES)


def _diff_sample_kernel(pt_ref, q_ref, *refs, pages_per_step, lam_init):
    del pt_ref
    k_refs = refs[:pages_per_step]
    v_refs = refs[pages_per_step:2 * pages_per_step]
    (knew_ref, vnew_ref, tbl_ref, lq1_ref, lk1_ref, lq2_ref, lk2_ref, sg_ref,
     o_ref, m_ref, l_ref, acc_ref) = refs[2 * pages_per_step:]
    j = pl.program_id(1)
    last_step = j == pl.num_programs(1) - 1
    q = q_ref[0]
    rows = q.shape[0]
    nq = o_ref.shape[1]
    groups = rows // nq

    @pl.when(j == 0)
    def _():
        m_ref[...] = jnp.full_like(m_ref, NEG)
        l_ref[...] = jnp.zeros_like(l_ref)
        acc_ref[...] = jnp.zeros_like(acc_ref)

    def update(scores, values):
        mx = None
        for s in scores:
            for t in range(s.shape[1] // LANES):
                piece = s[:, t * LANES:(t + 1) * LANES]
                mx = piece if mx is None else jnp.maximum(mx, piece)
        m_prev = m_ref[...]
        m = jnp.maximum(m_prev, jnp.max(mx, axis=-1, keepdims=True))
        alpha = jnp.exp(m_prev - m)
        lsum = alpha * l_ref[...]
        acc = alpha * acc_ref[...]
        for s, v in zip(scores, values):
            p = jnp.exp(s - m)
            for t in range(p.shape[1] // LANES):
                lsum = lsum + p[:, t * LANES:(t + 1) * LANES]
            acc = acc + jnp.dot(p.astype(BF16), v.astype(BF16), preferred_element_type=F32)
        m_ref[...] = m
        l_ref[...] = lsum
        acc_ref[...] = acc

    scores = []
    for t in range(pages_per_step):
        s = _dot_nt(q, k_refs[t][...].astype(BF16))
        if t == pages_per_step - 1:
            s = s + jnp.where(last_step, tbl_ref[1], tbl_ref[0])
        else:
            s = s + tbl_ref[0]
        scores.append(s)
    update(scores, [v[...] for v in v_refs])

    @pl.when(last_step)
    def _():
        update([_dot_nt(q, knew_ref[0].astype(BF16)) + tbl_ref[2]], [vnew_ref[0]])
        on = acc_ref[...] * (1.0 / jnp.sum(l_ref[...], axis=-1, keepdims=True))
        lam = _lambda(lq1_ref, lk1_ref, lq2_ref, lk2_ref, lam_init)
        g = lax.broadcasted_iota(jnp.int32, on.shape, 0) % groups
        for h in range(N_DIFF_HEADS):
            w = jnp.where(g == 2 * h, 1.0, jnp.where(g == 2 * h + 1, -lam, 0.0))
            o = jnp.sum((on * w).reshape(nq, groups, LANES), axis=1)
            o_ref[0, :, h * LANES:(h + 1) * LANES] = _sub_norm(o, sg_ref[...], lam_init)


def _diff_tables(rel_bias, n_new):
    col = jnp.arange(PAGE_WORDS, dtype=jnp.int32)
    key, key_head = col // N_DIFF_HEADS, col % N_DIFF_HEADS
    row = jnp.arange(n_new * 2 * N_DIFF_HEADS, dtype=jnp.int32)
    q, row_head = row // (2 * N_DIFF_HEADS), (row % (2 * N_DIFF_HEADS)) // 2
    d_far = jnp.full((row.shape[0], PAGE_WORDS), 2 * MAX_DISTANCE, jnp.int32)
    d_last = PAGE_ROWS + q[:, None] - key[None, :]
    d_new = jnp.where(key[None, :] < n_new, q[:, None] - key[None, :], -1)
    dist = jnp.stack([d_far, d_last, d_new])
    bias = _bias_of(rel_bias[:, N_MOBA_HEADS:], dist)
    own = jnp.take_along_axis(bias, jnp.broadcast_to(row_head[None, None, :, None], (1,) + dist.shape), axis=0)[0]
    return jnp.where(row_head[:, None] == key_head[None, :], own, NEG)


def _diff_sample(q_rows, kd, vd, knew, vnew, tables, page_table, lam_vecs, subln_g, lam_init, nq):
    batch, rows, _ = q_rows.shape
    n_pages = page_table.shape[1]
    pps = math.gcd(PAGES_PER_STEP, n_pages)
    page = lambda t: (lambda b, j, pt: (pt[b * n_pages + j * pps + t], 0))
    per_batch3 = lambda b, j, pt: (b, 0, 0)
    fixed2 = lambda b, j, pt: (0, 0)
    vec = pl.BlockSpec((1, HEAD_DIM), fixed2)
    page_specs = [pl.BlockSpec((PAGE_WORDS, LANES), page(t)) for t in range(pps)]
    grid_spec = pltpu.PrefetchScalarGridSpec(
        num_scalar_prefetch=1,
        grid=(batch, n_pages // pps),
        in_specs=[pl.BlockSpec((1, rows, LANES), per_batch3)] + page_specs + page_specs
                 + [pl.BlockSpec((1, PAGE_WORDS, LANES), per_batch3), pl.BlockSpec((1, PAGE_WORDS, LANES), per_batch3),
                    pl.BlockSpec((3, rows, PAGE_WORDS), lambda b, j, pt: (0, 0, 0)),
                    vec, vec, vec, vec, pl.BlockSpec((1, LANES), fixed2)],
        out_specs=pl.BlockSpec((1, nq, DIFF_W), per_batch3),
        scratch_shapes=[pltpu.VMEM((rows, 1), F32), pltpu.VMEM((rows, LANES), F32), pltpu.VMEM((rows, LANES), F32)],
    )
    return pl.pallas_call(
        functools.partial(_diff_sample_kernel, pages_per_step=pps, lam_init=lam_init),
        grid_spec=grid_spec,
        out_shape=jax.ShapeDtypeStruct((batch, nq, DIFF_W), F32),
        compiler_params=_params("parallel", "arbitrary"),
        name="diff_sample",
    )(page_table.reshape(-1), q_rows, *([kd] * pps), *([vd] * pps), knew, vnew, tables, *lam_vecs, subln_g)


def _sample_attention(qkv_s, new_rows, caches, page_table, rel_bias, lam_vecs, subln_g, lam_init):
    batch, nq, _ = qkv_s.shape
    cache_mk, cache_mv, cache_dk, cache_dv = caches
    km_new, vm_new, kd_new, vd_new = new_rows
    assert page_table.shape[1] % PAGES_PER_BLOCK == 0 and nq <= SUBLANES
    w = MOBA_W
    t_view = lambda c: jnp.transpose(c[0], (0, 2, 3, 1)).reshape(-1, PAGE_ROWS)
    r_view = lambda c: c[0].reshape(-1, LANES)
    lane_pad = lambda a: jnp.pad(a, ((0, 0), (0, 0), (0, LANES - a.shape[2])))
    row_pad = lambda a: jnp.pad(a, ((0, 0), (0, PAGE_WORDS - a.shape[1]), (0, 0)))

    qcol = jnp.broadcast_to(qkv_s[:, :, 0:w].astype(F32)[..., None], (batch, nq, w, LANES))
    sel = _moba_gate(qcol, t_view(cache_mk), page_table)
    bias_m = _sample_bias_rows(rel_bias, list(range(N_MOBA_HEADS)), nq)
    o_m = _moba_sample(sel, qcol, t_view(cache_mk), t_view(cache_mv),
                       lane_pad(jnp.swapaxes(km_new, 1, 2)), lane_pad(jnp.swapaxes(vm_new, 1, 2)), bias_m, page_table)

    qd = qkv_s[:, :, 3 * w:4 * w].reshape(batch, nq, N_DIFF_HEADS, 1, LANES)
    own_map = (jnp.arange(LANES, dtype=jnp.int32)[None, :] // HEAD_DIM) == jnp.arange(2, dtype=jnp.int32)[:, None]
    q_rows = jnp.where(own_map, qd, jnp.zeros((), qd.dtype)).reshape(batch, nq * 2 * N_DIFF_HEADS, LANES)
    o_d = _diff_sample(q_rows, r_view(cache_dk), r_view(cache_dv),
                       row_pad(kd_new.reshape(batch, nq * N_DIFF_HEADS, LANES)),
                       row_pad(vd_new.reshape(batch, nq * N_DIFF_HEADS, LANES)),
                       _diff_tables(rel_bias, nq), page_table, lam_vecs, subln_g, lam_init, nq)
    return jnp.concatenate([o_m, o_d], axis=-1)


def _post_attn_kernel(a_ref, x_ref, gi_ref, bi_ref, wo_ref, g1_ref, b1_ref, wrh_ref, wrl_ref, br_ref,
                      h1_ref, te_ref, tg_ref, *, alpha):
    h = _layer_norm(x_ref[...], gi_ref[...], bi_ref[...])
    mixed = jnp.dot(a_ref[...], wo_ref[...], preferred_element_type=F32)
    h1 = _layer_norm(alpha * h + mixed, g1_ref[...], b1_ref[...])
    h1_ref[...] = h1
    hi, lo = _split_bf16(h1)
    w_hi = wrh_ref[...]
    logits = (jnp.dot(hi, w_hi, preferred_element_type=F32) + jnp.dot(hi, wrl_ref[...], preferred_element_type=F32)
              + jnp.dot(lo, w_hi, preferred_element_type=F32) + br_ref[...])
    lane = lax.broadcasted_iota(jnp.int32, logits.shape, 1)
    work = jnp.where(lane < N_EXPERTS, logits, -jnp.inf)
    vals, ids = [], []
    for _ in range(TOP_K):
        mx = jnp.max(work, axis=-1, keepdims=True)
        first = jnp.min(jnp.where(work == mx, lane, LANES), axis=-1, keepdims=True)
        vals.append(mx)
        ids.append(first)
        work = jnp.where(lane == first, -jnp.inf, work)
    es = [jnp.exp(v - vals[0]) for v in vals]
    inv = 1.0 / (es[0] + es[1] + es[2] + es[3])
    te = jnp.zeros(logits.shape, jnp.int32)
    tg = jnp.zeros(logits.shape, F32)
    for k in range(TOP_K):
        te = jnp.where(lane == k, ids[k], te)
        tg = jnp.where(lane == k, es[k] * inv, tg)
    te_ref[...] = te
    tg_ref[...] = tg


def _post_attn(attn, x, ln_in, w_o_bf16, ln1, wr_hi, wr_lo, br, alpha):
    n, d = x.shape
    tm = min(POST_ATTN_ROWS, n)
    row = lambda i: (i, 0)
    fixed = lambda i: (0, 0)
    vec = pl.BlockSpec((1, d), fixed)
    return pl.pallas_call(
        functools.partial(_post_attn_kernel, alpha=alpha),
        grid=(n // tm,),
        in_specs=[pl.BlockSpec((tm, d), row), pl.BlockSpec((tm, d), row), vec, vec,
                  pl.BlockSpec((d, d), fixed), vec, vec,
                  pl.BlockSpec((d, LANES), fixed), pl.BlockSpec((d, LANES), fixed), pl.BlockSpec((1, LANES), fixed)],
        out_specs=[pl.BlockSpec((tm, d), row), pl.BlockSpec((tm, LANES), row), pl.BlockSpec((tm, LANES), row)],
        out_shape=[jax.ShapeDtypeStruct((n, d), F32), jax.ShapeDtypeStruct((n, LANES), jnp.int32),
                   jax.ShapeDtypeStruct((n, LANES), F32)],
        compiler_params=_params("parallel"),
        name="post_attn",
    )(attn, x, *ln_in, w_o_bf16, *ln1, wr_hi, wr_lo, br)


def _dispatch_tables(top_e, tm):
    n = top_e.shape[0]
    onehot = (top_e[:, :, None] == jnp.arange(N_EXPERTS, dtype=jnp.int32)).astype(jnp.int32).sum(axis=1)
    csum = jnp.cumsum(onehot, axis=0)
    counts = csum[-1]
    padded = (counts + tm - 1) // tm * tm
    pad_end = jnp.cumsum(padded)
    pad_start = pad_end - padded
    dest = pad_start[top_e] + jnp.take_along_axis(csum - onehot, top_e, axis=1)
    n_blocks = -(-(n * TOP_K) // tm) + N_EXPERTS
    tok = jnp.repeat(jnp.arange(n, dtype=jnp.int32), TOP_K)
    row_tok = jnp.zeros((n_blocks * tm,), jnp.int32).at[dest.reshape(-1)].set(tok)
    blk_e = jnp.minimum(jnp.searchsorted(pad_end, jnp.arange(n_blocks, dtype=jnp.int32) * tm, side='right'),
                        N_EXPERTS - 1).astype(jnp.int32)
    return dest.astype(jnp.int32), row_tok.reshape(n_blocks, 1, tm), blk_e


def _expert_kernel(be_ref, rt_ref, h_ref, wgu_ref, bgu_ref, wdn_ref, bdn_ref, o_ref,
                   idx_ref, xbuf_ref, isem, rsem, *, tm, n_blocks):
    del be_ref
    i = pl.program_id(0)
    slot = i % 2
    nxt = 1 - slot

    def idx_copy(blk, s):
        return pltpu.make_async_copy(rt_ref.at[blk, 0], idx_ref.at[s], isem.at[s])

    def issue_rows(s):
        def body(r, carry):
            tok = idx_ref[s, r]
            pltpu.make_async_copy(h_ref.at[pl.ds(tok, 1), :], xbuf_ref.at[s, pl.ds(r, 1), :], rsem.at[s]).start()
            return carry
        lax.fori_loop(0, tm, body, 0, unroll=ISSUE_UNROLL)

    @pl.when(i == 0)
    def _():
        first = idx_copy(0, 0)
        first.start()
        first.wait()
        issue_rows(0)
        if n_blocks > 1:
            idx_copy(1, 1).start()

    @pl.when(i + 1 < n_blocks)
    def _():
        idx_copy(i + 1, nxt).wait()
        issue_rows(nxt)

    @pl.when(i + 2 < n_blocks)
    def _():
        idx_copy(i + 2, slot).start()

    pltpu.make_async_copy(h_ref.at[pl.ds(0, tm), :], xbuf_ref.at[slot], rsem.at[slot]).wait()

    x = xbuf_ref[slot].astype(BF16)
    gu = jnp.dot(x, wgu_ref[0], preferred_element_type=F32) + bgu_ref[0]
    f = gu.shape[1] // 2
    g = jnp.minimum(gu[:, :f], SWIGLU_LIMIT)
    u = jnp.clip(gu[:, f:], -SWIGLU_LIMIT, SWIGLU_LIMIT)
    act = (u + 1.0) * g * (1.0 / (1.0 + jnp.exp(-SWIGLU_ALPHA * g)))
    o_ref[...] = jnp.dot(act.astype(BF16), wdn_ref[0], preferred_element_type=F32) + bdn_ref[0]


def _experts(h1, row_tok, blk_e, w_gu, b_gu, w_dn, b_dn):
    n, d = h1.shape
    n_blocks, _, tm = row_tok.shape
    f2 = w_gu.shape[2]
    by_expert3 = lambda i, be: (be[i], 0, 0)
    grid_spec = pltpu.PrefetchScalarGridSpec(
        num_scalar_prefetch=1,
        grid=(n_blocks,),
        in_specs=[pl.BlockSpec(memory_space=pltpu.VMEM), pl.BlockSpec(memory_space=pl.ANY),
                  pl.BlockSpec((1, d, f2), by_expert3), pl.BlockSpec((1, 1, f2), by_expert3),
                  pl.BlockSpec((1, f2 // 2, d), by_expert3), pl.BlockSpec((1, 1, d), by_expert3)],
        out_specs=pl.BlockSpec((tm, d), lambda i, be: (i, 0)),
        scratch_shapes=[pltpu.SMEM((2, tm), jnp.int32), pltpu.VMEM((2, tm, d), F32),
                        pltpu.SemaphoreType.DMA((2,)), pltpu.SemaphoreType.DMA((2,))],
    )
    return pl.pallas_call(
        functools.partial(_expert_kernel, tm=tm, n_blocks=n_blocks),
        grid_spec=grid_spec,
        out_shape=jax.ShapeDtypeStruct((n_blocks * tm, d), F32),
        compiler_params=_params("arbitrary"),
        name="experts",
    )(blk_e, row_tok, h1, w_gu, b_gu, w_dn, b_dn)


def _combine_kernel(dt_ref, r_ref, h1_ref, tg_ref, g2_ref, b2_ref, o_ref, idx_ref, rbuf_ref, isem, rsem,
                    *, tm, n_tiles, alpha):
    i = pl.program_id(0)
    slot = i % 2
    nxt = 1 - slot

    def idx_copy(tile, s):
        return pltpu.make_async_copy(dt_ref.at[tile, 0], idx_ref.at[s], isem.at[s])

    def issue_rows(s):
        for k in range(TOP_K):
            def body(r, carry):
                src = idx_ref[s, k * tm + r]
                pltpu.make_async_copy(r_ref.at[pl.ds(src, 1), :], rbuf_ref.at[s, k, pl.ds(r, 1), :],
                                      rsem.at[s]).start()
                return carry
            lax.fori_loop(0, tm, body, 0, unroll=ISSUE_UNROLL)

    @pl.when(i == 0)
    def _():
        first = idx_copy(0, 0)
        first.start()
        first.wait()
        issue_rows(0)
        if n_tiles > 1:
            idx_copy(1, 1).start()

    @pl.when(i + 1 < n_tiles)
    def _():
        idx_copy(i + 1, nxt).wait()
        issue_rows(nxt)

    @pl.when(i + 2 < n_tiles)
    def _():
        idx_copy(i + 2, slot).start()

    for k in range(TOP_K):
        pltpu.make_async_copy(r_ref.at[pl.ds(0, tm), :], rbuf_ref.at[slot, k], rsem.at[slot]).wait()

    tg = tg_ref[...]
    moe = tg[:, 0:1] * rbuf_ref[slot, 0]
    for k in range(1, TOP_K):
        moe = moe + tg[:, k:k + 1] * rbuf_ref[slot, k]
    o_ref[...] = _layer_norm(alpha * h1_ref[...] + moe, g2_ref[...], b2_ref[...])


def _combine(dest, rows, h1, tg, ln2, alpha):
    n, d = h1.shape
    tm = min(COMBINE_ROWS, n)
    n_tiles = n // tm
    dt = dest.reshape(n_tiles, tm, TOP_K).swapaxes(1, 2).reshape(n_tiles, 1, TOP_K * tm)
    row = lambda i: (i, 0)
    fixed = lambda i: (0, 0)
    vec = pl.BlockSpec((1, d), fixed)
    return pl.pallas_call(
        functools.partial(_combine_kernel, tm=tm, n_tiles=n_tiles, alpha=alpha),
        grid=(n_tiles,),
        in_specs=[pl.BlockSpec(memory_space=pltpu.VMEM), pl.BlockSpec(memory_space=pl.ANY),
                  pl.BlockSpec((tm, d), row), pl.BlockSpec((tm, LANES), row), vec, vec],
        out_specs=pl.BlockSpec((tm, d), row),
        out_shape=jax.ShapeDtypeStruct((n, d), F32),
        scratch_shapes=[pltpu.SMEM((2, TOP_K * tm), jnp.int32), pltpu.VMEM((2, TOP_K, tm, d), F32),
                        pltpu.SemaphoreType.DMA((2,)), pltpu.SemaphoreType.DMA((2,))],
        compiler_params=_params("arbitrary"),
        name="combine",
    )(dt, rows, h1, tg, *ln2)


def _moe_and_norm(h1, top_e, tg, moe_w, ln2, alpha):
    dest, row_tok, blk_e = _dispatch_tables(top_e[:, :TOP_K], EXPERT_ROWS)
    rows = _experts(h1, row_tok, blk_e, *moe_w)
    return _combine(dest, rows, h1, tg, ln2, alpha)


def kernel(x_prompt, x_sample, cache_moba_k, cache_moba_v, cache_diff_k, cache_diff_v, page_table, ln_in_g, ln_in_b, w_in, w_o, lambda_q1, lambda_k1, lambda_q2, lambda_k2, subln_g, rel_bias, ln1_g, ln1_b, w_router, b_router, w_gate_up, b_gate_up, w_down, b_down, ln2_g, ln2_b):
    depth = w_in.shape[0]
    assert depth == 1, "single-layer trunk only"
    batch, seq, d = x_prompt.shape
    dec_batch, dec_seq, _ = x_sample.shape
    assert seq % MOBA_BLOCK == 0
    alpha = (2 * depth) ** 0.25
    lam_init = 0.8 - 0.6 * math.exp(-0.3 * 0)
    vec = lambda a: a.reshape(1, -1)

    ln_in = (vec(ln_in_g), vec(ln_in_b))
    ln1 = (vec(ln1_g[0]), vec(ln1_b[0]))
    ln2 = (vec(ln2_g[0]), vec(ln2_b[0]))
    w_in_b = w_in[0].astype(BF16)
    w_o_b = w_o[0].astype(BF16)
    lam_vecs = (vec(lambda_q1[0]), vec(lambda_k1[0]), vec(lambda_q2[0]), vec(lambda_k2[0]))
    sg = vec(subln_g[0])
    pad_e = ((0, 0), (0, LANES - N_EXPERTS))
    wr_hi, wr_lo = _split_bf16(jnp.pad(w_router[0], pad_e))
    br = jnp.pad(vec(b_router[0]), pad_e)
    moe_w = (w_gate_up[0].astype(BF16), b_gate_up[0][:, None, :], w_down[0].astype(BF16), b_down[0][:, None, :])

    xp = x_prompt.reshape(batch * seq, d)
    qkv_p, km_p, vm_p, kd_p, vd_p = _in_proj(xp, *ln_in, w_in_b)
    tiles, far = _prompt_bias_tiles(rel_bias, MOBA_BLOCK)
    attn_p = _prompt_attention(qkv_p.reshape(batch, seq, -1), tiles, far, lam_vecs, sg, lam_init, batch, seq)
    h1_p, te_p, tg_p = _post_attn(attn_p.reshape(batch * seq, d), xp, ln_in, w_o_b, ln1, wr_hi, wr_lo, br, alpha)
    y_p = _moe_and_norm(h1_p, te_p, tg_p, moe_w, ln2, alpha)

    xs = x_sample.reshape(dec_batch * dec_seq, d)
    qkv_s, km_s, vm_s, kd_s, vd_s = _in_proj(xs, *ln_in, w_in_b)
    new_rows = [a.reshape(dec_batch, dec_seq, MOBA_W) for a in (km_s, vm_s, kd_s, vd_s)]
    attn_s = _sample_attention(qkv_s.reshape(dec_batch, dec_seq, -1), new_rows,
                               (cache_moba_k, cache_moba_v, cache_diff_k, cache_diff_v), page_table, rel_bias,
                               lam_vecs, sg, lam_init)
    attn_s = attn_s.reshape(dec_batch * dec_seq, d).astype(BF16)
    h1_s, te_s, tg_s = _post_attn(attn_s, xs, ln_in, w_o_b, ln1, wr_hi, wr_lo, br, alpha)
    y_s = _moe_and_norm(h1_s, te_s, tg_s, moe_w, ln2, alpha)

    mk = lambda a, b_, t: a.reshape(1, b_, t, N_MOBA_HEADS, HEAD_DIM)
    dk = lambda a, b_, t: a.reshape(1, b_, t, N_DIFF_HEADS, 2 * HEAD_DIM)
    return (y_p.reshape(batch, seq, d), y_s.reshape(dec_batch, dec_seq, d),
            mk(km_p, batch, seq), mk(vm_p, batch, seq), dk(kd_p, batch, seq), dk(vd_p, batch, seq),
            mk(km_s, dec_batch, dec_seq), mk(vm_s, dec_batch, dec_seq),
            dk(kd_s, dec_batch, dec_seq), dk(vd_s, dec_batch, dec_seq))
```
